```python
import jax, jax.numpy as jnp
from jax import lax
import numpy as np

D_MODEL = 2048
BATCH = 2
SEQ = 4096
DEPTH = 4

N_MIXERS = 2
N_SB_LAYERS = (DEPTH + 1) // 2
N_GM_LAYERS = DEPTH // 2
SB_HEADS = 16
SB_HEAD_DIM = D_MODEL // SB_HEADS
Q_BLOCK = 128
GM_CHUNK = 128
GM_FFN = 6 * D_MODEL
GM_HALF = GM_FFN // 2
GM_GROUPS = 16
GM_GROUP_DIM = GM_HALF // GM_GROUPS
N_GROUPS = 4
EXPERTS_PER_GROUP = 8
TOP_K = 2
D_EXPERT = 512
LN_EPS = 1e-5
DEEPNORM_ALPHA = (2.0 * DEPTH) ** 0.25
DEEPNORM_BETA = (8.0 * DEPTH) ** -0.25

kernel_name = "hybrid_stickbreak_gmlp_hmoe_deepnorm"


def layer_norm(x, g, b):
    xf = x.astype(jnp.float32)
    mu = jnp.mean(xf, axis=-1, keepdims=True)
    var = jnp.mean(jnp.square(xf - mu), axis=-1, keepdims=True)
    y = (xf - mu) * lax.rsqrt(var + LN_EPS) * g.astype(jnp.float32) + b.astype(jnp.float32)
    return y.astype(x.dtype)


def stick_breaking_attention(x, w_qkv, w_o):
    B, S, _ = x.shape
    qkv = x @ w_qkv
    q, k, v = jnp.split(qkv, 3, axis=-1)
    to_heads = lambda a: a.reshape(B, S, SB_HEADS, SB_HEAD_DIM).transpose(0, 2, 1, 3)
    q, k, v = to_heads(q), to_heads(k), to_heads(v)
    n_blk = S // Q_BLOCK
    q_blocks = q.reshape(B, SB_HEADS, n_blk, Q_BLOCK, SB_HEAD_DIM).transpose(2, 0, 1, 3, 4)
    key_pos = jnp.arange(S)
    scale = SB_HEAD_DIM ** -0.5

    def one_block(args):
        q_blk, blk_idx = args
        q_pos = blk_idx * Q_BLOCK + jnp.arange(Q_BLOCK)
        z = jnp.einsum('bhqd,bhkd->bhqk', q_blk, k).astype(jnp.float32) * scale
        causal = key_pos[None, :] < q_pos[:, None]
        log_beta = jax.nn.log_sigmoid(z)
        log_one_minus = jnp.where(causal, jax.nn.log_sigmoid(-z), 0.0)
        log_rest = lax.cumsum(log_one_minus, axis=log_one_minus.ndim - 1, reverse=True) - log_one_minus
        att = jnp.where(causal, jnp.exp(log_beta + log_rest), 0.0)
        return jnp.einsum('bhqk,bhkd->bhqd', att.astype(v.dtype), v)

    o = lax.map(one_block, (q_blocks, jnp.arange(n_blk)))
    o = o.transpose(1, 0, 3, 2, 4).reshape(B, S, D_MODEL)
    return o @ w_o


def chunked_spatial_gating(x, w_in, v_ln_g, v_ln_b, w_spatial, b_spatial, w_out):
    B, S, _ = x.shape
    z = jax.nn.gelu(x @ w_in)
    u, v = jnp.split(z, 2, axis=-1)
    v = layer_norm(v, v_ln_g, v_ln_b)
    n_ch = S // GM_CHUNK
    v = v.reshape(B, n_ch, GM_CHUNK, GM_GROUPS, GM_GROUP_DIM)
    tri = jnp.tril(jnp.ones((GM_CHUNK, GM_CHUNK), dtype=bool))
    w_s = jnp.where(tri[None], w_spatial, 0.0)
    s = jnp.einsum('gts,bcsgd->bctgd', w_s.astype(v.dtype), v) + b_spatial.T[:, :, None]
    s = s.reshape(B, S, GM_HALF)
    return (u * s) @ w_out


def hierarchical_moe(x, w_group_router, b_group_router, w_expert_router, b_expert_router,
                     w_gate, w_up, w_down):
    B, S, D = x.shape
    t = x.reshape(-1, D)
    group_logits = (t @ w_group_router).astype(jnp.float32) + b_group_router.astype(jnp.float32)
    group_probs = jax.nn.softmax(group_logits, axis=-1)
    g_sel = jnp.argmax(group_logits, axis=-1)
    p_group = jnp.take_along_axis(group_probs, g_sel[:, None], axis=-1)
    inner_all = jnp.einsum('nd,gde->nge', t, w_expert_router).astype(jnp.float32) \
        + b_expert_router.astype(jnp.float32)
    inner = jnp.take_along_axis(inner_all, g_sel[:, None, None], axis=1)[:, 0]
    top_vals, top_idx = lax.top_k(inner, TOP_K)
    top_w = jax.nn.softmax(top_vals, axis=-1)
    gate_inner = jnp.sum(jax.nn.one_hot(top_idx, EXPERTS_PER_GROUP, dtype=jnp.float32)
                         * top_w[..., None], axis=1)
    gate_group = jax.nn.one_hot(g_sel, N_GROUPS, dtype=jnp.float32) * p_group
    gates = (gate_group[:, :, None] * gate_inner[:, None, :]).astype(t.dtype)
    y = jnp.zeros_like(t)
    for gi in range(N_GROUPS):
        h = jax.nn.silu(jnp.einsum('nd,edf->nef', t, w_gate[gi])) \
            * jnp.einsum('nd,edf->nef', t, w_up[gi])
        h = h * gates[:, gi, :, None]
        y = y + jnp.einsum('nef,efd->nd', h, w_down[gi])
    return y.reshape(B, S, D)


def setup_inputs(seed: int = 0) -> dict:
    key = jax.random.key(seed)
    ks = jax.random.split(key, 24)
    nrm = lambda k, shape, s: jax.random.normal(k, shape, jnp.float32) * s
    D, E, G, F = D_MODEL, EXPERTS_PER_GROUP, N_GROUPS, D_EXPERT
    x = nrm(ks[0], (BATCH, SEQ, D), 1.0)
    qkv_scale = jnp.concatenate([jnp.ones((2 * D,), jnp.float32),
                                 jnp.full((D,), DEEPNORM_BETA, jnp.float32)])
    sb_w_qkv = nrm(ks[1], (N_SB_LAYERS, D, 3 * D), D ** -0.5) * qkv_scale
    sb_w_o = nrm(ks[2], (N_SB_LAYERS, D, D), D ** -0.5 * DEEPNORM_BETA)
    gm_w_in = nrm(ks[3], (N_GM_LAYERS, D, GM_FFN), D ** -0.5)
    gm_v_ln_g = 1.0 + nrm(ks[4], (N_GM_LAYERS, GM_HALF), 0.02)
    gm_v_ln_b = nrm(ks[5], (N_GM_LAYERS, GM_HALF), 0.02)
    gm_w_spatial = nrm(ks[6], (N_GM_LAYERS, GM_GROUPS, GM_CHUNK, GM_CHUNK), 0.5 * GM_CHUNK ** -0.5)
    gm_b_spatial = 1.0 + nrm(ks[7], (N_GM_LAYERS, GM_GROUPS, GM_CHUNK), 0.02)
    gm_w_out = nrm(ks[8], (N_GM_LAYERS, GM_HALF, D), GM_HALF ** -0.5 * DEEPNORM_BETA)
    mix_ln_g = 1.0 + nrm(ks[9], (DEPTH, D), 0.02)
    mix_ln_b = nrm(ks[10], (DEPTH, D), 0.02)
    moe_w_group_router = nrm(ks[11], (DEPTH, D, G), D ** -0.5)
    moe_b_group_router = nrm(ks[12], (DEPTH, G), 0.01)
    moe_w_expert_router = nrm(ks[13], (DEPTH, G, D, E), D ** -0.5)
    moe_b_expert_router = nrm(ks[14], (DEPTH, G, E), 0.01)
    moe_w_gate = nrm(ks[15], (DEPTH, G, E, D, F), D ** -0.5)
    moe_w_up = nrm(ks[16], (DEPTH, G, E, D, F), D ** -0.5)
    moe_w_down = nrm(ks[17], (DEPTH, G, E, F, D), F ** -0.5 * DEEPNORM_BETA)
    ffn_ln_g = 1.0 + nrm(ks[18], (DEPTH, D), 0.02)
    ffn_ln_b = nrm(ks[19], (DEPTH, D), 0.02)
    return {"x": x, "sb_w_qkv": sb_w_qkv, "sb_w_o": sb_w_o,
            "gm_w_in": gm_w_in, "gm_v_ln_g": gm_v_ln_g, "gm_v_ln_b": gm_v_ln_b,
            "gm_w_spatial": gm_w_spatial, "gm_b_spatial": gm_b_spatial, "gm_w_out": gm_w_out,
            "mix_ln_g": mix_ln_g, "mix_ln_b": mix_ln_b,
            "moe_w_group_router": moe_w_group_router, "moe_b_group_router": moe_b_group_router,
            "moe_w_expert_router": moe_w_expert_router, "moe_b_expert_router": moe_b_expert_router,
            "moe_w_gate": moe_w_gate, "moe_w_up": moe_w_up, "moe_w_down": moe_w_down,
            "ffn_ln_g": ffn_ln_g, "ffn_ln_b": ffn_ln_b}


def reference(x, sb_w_qkv, sb_w_o, gm_w_in, gm_v_ln_g, gm_v_ln_b, gm_w_spatial, gm_b_spatial,
              gm_w_out, mix_ln_g, mix_ln_b, moe_w_group_router, moe_b_group_router,
              moe_w_expert_router, moe_b_expert_router, moe_w_gate, moe_w_up, moe_w_down,
              ffn_ln_g, ffn_ln_b):
    for i in range(DEPTH):
        j = i // N_MIXERS
        if i % N_MIXERS == 0:
            h = stick_breaking_attention(x, sb_w_qkv[j], sb_w_o[j])
        else:
            h = chunked_spatial_gating(x, gm_w_in[j], gm_v_ln_g[j], gm_v_ln_b[j],
                                       gm_w_spatial[j], gm_b_spatial[j], gm_w_out[j])
        x = layer_norm(DEEPNORM_ALPHA * x + h, mix_ln_g[i], mix_ln_b[i])
        f = hierarchical_moe(x, moe_w_group_router[i], moe_b_group_router[i],
                             moe_w_expert_router[i], moe_b_expert_router[i],
                             moe_w_gate[i], moe_w_up[i], moe_w_down[i])
        x = layer_norm(DEEPNORM_ALPHA * x + f, ffn_ln_g[i], ffn_ln_b[i])
    return x
```

```python
import functools

import jax
import jax.numpy as jnp
from jax import lax
from jax.experimental import pallas as pl
from jax.experimental.pallas import tpu as pltpu

D_MODEL = 2048
BATCH = 2
SEQ = 4096
DEPTH = 4
N_TOK = BATCH * SEQ
N_MIXERS = 2
SB_HEADS = 16
HEAD_DIM = D_MODEL // SB_HEADS
GM_CHUNK = 128
GM_FFN = 6 * D_MODEL
GM_HALF = GM_FFN // 2
GM_GROUPS = 16
GM_GROUP_DIM = GM_HALF // GM_GROUPS
N_GROUPS = 4
EXPERTS_PER_GROUP = 8
N_EXPERTS = N_GROUPS * EXPERTS_PER_GROUP
TOP_K = 2
D_EXPERT = 512
LN_EPS = 1e-5
DEEPNORM_ALPHA = (2.0 * DEPTH) ** 0.25

LANES = 128
VMEM_LIMIT = 56 * 1024 * 1024

ATT_TQ = 256
LN_TM = 256
EXP_TM = 256
N_ROUTE_COLS = N_GROUPS + N_EXPERTS
MAX_TILES = (N_TOK * TOP_K) // EXP_TM + N_EXPERTS
SORTED_ROWS = MAX_TILES * EXP_TM
TOK_ROWS = N_TOK // LANES
DISPATCH_CHUNK = 256

f32 = jnp.float32
bf16 = jnp.bfloat16
i32 = jnp.int32


def _params(*sem):
    return pltpu.CompilerParams(dimension_semantics=sem, vmem_limit_bytes=VMEM_LIMIT)


def _mm_kernel(x_ref, w_ref, o_ref, wbf_ref, *, act):
    @pl.when(pl.program_id(1) == 0)
    def _():
        wbf_ref[...] = w_ref[...].astype(bf16)

    acc = jnp.dot(x_ref[...].astype(bf16), wbf_ref[...], preferred_element_type=f32)
    if act == "gelu":
        acc = jax.nn.gelu(acc)
    o_ref[...] = acc.astype(o_ref.dtype)


def matmul_wcast(x, w_stack, layer, *, tm, tn, out_dtype, act=None, name):
    m, k = x.shape
    n = w_stack.shape[-1]
    return pl.pallas_call(
        functools.partial(_mm_kernel, act=act),
        grid=(n // tn, m // tm),
        in_specs=[
            pl.BlockSpec((tm, k), lambda j, i: (i, 0)),
            pl.BlockSpec((None, k, tn), lambda j, i: (layer, 0, j)),
        ],
        out_specs=pl.BlockSpec((tm, tn), lambda j, i: (i, j)),
        out_shape=jax.ShapeDtypeStruct((m, n), out_dtype),
        scratch_shapes=[pltpu.VMEM((k, tn), bf16)],
        compiler_params=_params("arbitrary", "arbitrary"),
        name=name,
    )(x, w_stack)


def _attn_kernel(q_ref, k_ref, v_ref, m_ref, o_ref, acc_ref, carry_ref):
    qi = pl.program_id(2)
    q = q_ref[...]
    scale = HEAD_DIM ** -0.5

    def tile(j, diagonal):
        start = pl.multiple_of(j * ATT_TQ, ATT_TQ)
        k = k_ref[pl.ds(start, ATT_TQ), :]
        v = v_ref[pl.ds(start, ATT_TQ), :]
        z = lax.dot_general(q, k, (((1,), (1,)), ((), ())), preferred_element_type=f32) * scale
        log_beta = jnp.minimum(z, 0.0) - jnp.log1p(jnp.exp(-jnp.abs(z)))
        log_rest_term = log_beta - z
        if diagonal:
            row = lax.broadcasted_iota(i32, (ATT_TQ, ATT_TQ), 0)
            col = lax.broadcasted_iota(i32, (ATT_TQ, ATT_TQ), 1)
            causal = col < row
            log_rest_term = jnp.where(causal, log_rest_term, 0.0)
        hi = log_rest_term.astype(bf16)
        lo = (log_rest_term - hi.astype(f32)).astype(bf16)
        m = m_ref[...]
        rest = (jnp.dot(hi, m, preferred_element_type=f32)
                + jnp.dot(lo, m, preferred_element_type=f32))
        block_total = rest[:, 0:1] + log_rest_term[:, 0:1]
        if diagonal:
            att = jnp.where(causal, jnp.exp(log_beta + rest), 0.0)
            acc_ref[...] = jnp.dot(att.astype(bf16), v, preferred_element_type=f32)
            carry_ref[...] = block_total
        else:
            att = jnp.exp(log_beta + rest + carry_ref[...])
            acc_ref[...] += jnp.dot(att.astype(bf16), v, preferred_element_type=f32)
            carry_ref[...] += block_total

    tile(qi, True)

    def body(i, c):
        tile(qi - 1 - i, False)
        return c

    lax.fori_loop(0, qi, body, 0)
    o_ref[...] = acc_ref[...].astype(o_ref.dtype)


def stick_breaking_attention(qkv, tri):
    nq = SEQ // ATT_TQ
    return pl.pallas_call(
        _attn_kernel,
        grid=(BATCH, SB_HEADS, nq),
        in_specs=[
            pl.BlockSpec((ATT_TQ, HEAD_DIM), lambda b, h, i: (b * nq + i, h)),
            pl.BlockSpec((SEQ, HEAD_DIM), lambda b, h, i: (b, SB_HEADS + h)),
            pl.BlockSpec((SEQ, HEAD_DIM), lambda b, h, i: (b, 2 * SB_HEADS + h)),
            pl.BlockSpec((ATT_TQ, ATT_TQ), lambda b, h, i: (0, 0)),
        ],
        out_specs=pl.BlockSpec((ATT_TQ, HEAD_DIM), lambda b, h, i: (b * nq + i, h)),
        out_shape=jax.ShapeDtypeStruct((N_TOK, D_MODEL), bf16),
        scratch_shapes=[pltpu.VMEM((ATT_TQ, HEAD_DIM), f32), pltpu.VMEM((ATT_TQ, 1), f32)],
        compiler_params=_params("arbitrary", "arbitrary", "arbitrary"),
        name="sb_attention",
    )(qkv, qkv, qkv, tri)


def _gate_kernel(u_ref, v_ref, g_ref, b_ref, ws_ref, bs_ref, o_ref):
    v = v_ref[...]
    mu = jnp.mean(v, axis=-1, keepdims=True)
    var = jnp.mean(jnp.square(v - mu), axis=-1, keepdims=True)
    rstd = lax.rsqrt(var + LN_EPS)
    row = lax.broadcasted_iota(i32, (GM_CHUNK, GM_CHUNK), 0)
    col = lax.broadcasted_iota(i32, (GM_CHUNK, GM_CHUNK), 1)
    lower = col <= row
    for g in range(GM_GROUPS):
        sl = slice(g * GM_GROUP_DIM, (g + 1) * GM_GROUP_DIM)
        vn = (v_ref[:, sl] - mu) * rstd * g_ref[:, sl] + b_ref[:, sl]
        w = jnp.where(lower, ws_ref[g], 0.0).astype(bf16)
        s = jnp.dot(w, vn.astype(bf16), preferred_element_type=f32) + bs_ref[:, g:g + 1]
        o_ref[:, sl] = (u_ref[:, sl] * s).astype(o_ref.dtype)


def spatial_gate(z, ln_g, ln_b, w_spatial, b_spatial_t, layer):
    n_chunks = N_TOK // GM_CHUNK
    return pl.pallas_call(
        _gate_kernel,
        grid=(n_chunks,),
        in_specs=[
            pl.BlockSpec((GM_CHUNK, GM_HALF), lambda c: (c, 0)),
            pl.BlockSpec((GM_CHUNK, GM_HALF), lambda c: (c, 1)),
            pl.BlockSpec((None, 1, GM_HALF), lambda c: (layer, 0, 0)),
            pl.BlockSpec((None, 1, GM_HALF), lambda c: (layer, 0, 0)),
            pl.BlockSpec((None, GM_GROUPS, GM_CHUNK, GM_CHUNK), lambda c: (layer, 0, 0, 0)),
            pl.BlockSpec((None, GM_CHUNK, GM_GROUPS), lambda c: (layer, 0, 0)),
        ],
        out_specs=pl.BlockSpec((GM_CHUNK, GM_HALF), lambda c: (c, 0)),
        out_shape=jax.ShapeDtypeStruct((N_TOK, GM_HALF), bf16),
        compiler_params=_params("arbitrary"),
        name="gm_spatial_gate",
    )(z, z, ln_g, ln_b, w_spatial, b_spatial_t)


def _split_bf16(a):
    hi = a.astype(bf16)
    lo = (a - hi.astype(f32)).astype(bf16)
    return hi, lo


def _ln_router_kernel(x_ref, h_ref, g_ref, b_ref, wr_ref, br_ref,
                      x1_ref, rf_ref, ri_ref, wh_ref, wl_ref):
    @pl.when(pl.program_id(0) == 0)
    def _():
        hi, lo = _split_bf16(wr_ref[...])
        wh_ref[...] = hi
        wl_ref[...] = lo

    y = DEEPNORM_ALPHA * x_ref[...] + h_ref[...]
    mu = jnp.mean(y, axis=-1, keepdims=True)
    var = jnp.mean(jnp.square(y - mu), axis=-1, keepdims=True)
    x1 = (y - mu) * lax.rsqrt(var + LN_EPS) * g_ref[...] + b_ref[...]
    x1_ref[...] = x1

    xh, xl = _split_bf16(x1)
    wh = wh_ref[...]
    logits = (jnp.dot(xh, wh, preferred_element_type=f32)
              + jnp.dot(xl, wh, preferred_element_type=f32)
              + jnp.dot(xh, wl_ref[...], preferred_element_type=f32)) + br_ref[...]

    lane = lax.broadcasted_iota(i32, logits.shape, 1)
    lane_f = lane.astype(f32)
    neg_inf = jnp.float32(-jnp.inf)
    big = jnp.float32(1e9)

    def masked_argmax(mask):
        vals = jnp.where(mask, logits, neg_inf)
        best = jnp.max(vals, axis=-1, keepdims=True)
        idx = jnp.min(jnp.where(mask & (logits == best), lane_f, big), axis=-1, keepdims=True)
        return best, idx

    gmask = lane < N_GROUPS
    gmax, gidx = masked_argmax(gmask)
    denom = jnp.sum(jnp.where(gmask, jnp.exp(logits - gmax), 0.0), axis=-1, keepdims=True)
    p_group = 1.0 / denom

    first = N_GROUPS + EXPERTS_PER_GROUP * gidx
    emask = (lane_f >= first) & (lane_f < first + EXPERTS_PER_GROUP)
    v1, i1 = masked_argmax(emask)
    v2, i2 = masked_argmax(emask & (lane_f != i1))
    t = jnp.exp(v2 - v1)
    w0 = p_group * (1.0 / (1.0 + t))
    w1 = p_group * (t / (1.0 + t))

    rf_ref[...] = jnp.where(lane == 0, w0, jnp.where(lane == 1, w1, 0.0))
    e0 = (i1 - N_GROUPS).astype(i32)
    e1 = (i2 - N_GROUPS).astype(i32)
    ri_ref[...] = jnp.where(lane == 0, e0, jnp.where(lane == 1, e1, 0))


def ln_router(x, h, ln_g, ln_b, w_route, b_route, layer):
    row = pl.BlockSpec((LN_TM, D_MODEL), lambda i: (i, 0))
    vec = pl.BlockSpec((None, 1, D_MODEL), lambda i: (layer, 0, 0))
    small = pl.BlockSpec((LN_TM, LANES), lambda i: (i, 0))
    return pl.pallas_call(
        _ln_router_kernel,
        grid=(N_TOK // LN_TM,),
        in_specs=[row, row, vec, vec,
                  pl.BlockSpec((None, D_MODEL, LANES), lambda i: (layer, 0, 0)),
                  pl.BlockSpec((None, 1, LANES), lambda i: (layer, 0, 0))],
        out_specs=[row, small, small],
        out_shape=[jax.ShapeDtypeStruct((N_TOK, D_MODEL), f32),
                   jax.ShapeDtypeStruct((N_TOK, LANES), f32),
                   jax.ShapeDtypeStruct((N_TOK, LANES), i32)],
        scratch_shapes=[pltpu.VMEM((D_MODEL, LANES), bf16), pltpu.VMEM((D_MODEL, LANES), bf16)],
        compiler_params=_params("arbitrary"),
        name="ln_router",
    )(x, h, ln_g, ln_b, w_route, b_route)


def _plan_kernel(e0_ref, e1_ref, d0_ref, d1_ref, info_ref):
    e0 = e0_ref[...]
    e1 = e1_ref[...]
    r = lax.broadcasted_iota(i32, (LANES, LANES), 0)
    c = lax.broadcasted_iota(i32, (LANES, LANES), 1)
    incl = (r <= c).astype(bf16)
    ones_l = jnp.ones((LANES, LANES), bf16)
    rr = lax.broadcasted_iota(i32, (TOK_ROWS, TOK_ROWS), 0)
    cc = lax.broadcasted_iota(i32, (TOK_ROWS, TOK_ROWS), 1)
    before = (cc < rr).astype(bf16)
    ones_r = jnp.ones((TOK_ROWS, TOK_ROWS), bf16)
    tile_start = (lax.broadcasted_iota(i32, (8, LANES), 1) * EXP_TM).astype(f32)

    off = jnp.zeros((TOK_ROWS, LANES), f32)
    d0 = jnp.zeros((TOK_ROWS, LANES), f32)
    d1 = jnp.zeros((TOK_ROWS, LANES), f32)
    tile_expert = jnp.zeros((8, LANES), f32)
    for e in range(N_EXPERTS):
        m0 = e0 == e
        m1 = e1 == e
        m = (m0 | m1).astype(f32).astype(bf16)
        within = jnp.dot(m, incl, preferred_element_type=f32)
        row_sum = jnp.dot(m, ones_l, preferred_element_type=f32).astype(bf16)
        above = jnp.dot(before, row_sum, preferred_element_type=f32)
        total = jnp.dot(ones_r, row_sum, preferred_element_type=f32)
        pos = off + above + within - 1.0
        d0 = jnp.where(m0, pos, d0)
        d1 = jnp.where(m1, pos, d1)
        padded = jnp.floor((total + (EXP_TM - 1)) * (1.0 / EXP_TM)) * EXP_TM
        off = off + padded
        tile_expert = tile_expert + (tile_start >= off[0:8, :]).astype(f32)
    d0_ref[...] = d0.astype(i32)
    d1_ref[...] = d1.astype(i32)
    n_tiles = off[0:8, :] * (1.0 / EXP_TM)
    sub = lax.broadcasted_iota(i32, (8, LANES), 0)
    info = jnp.where(sub == 0, jnp.minimum(tile_expert, N_EXPERTS - 1.0), n_tiles)
    info_ref[...] = info.astype(i32)


def routing_plan(e0, e1):
    full = pl.BlockSpec((TOK_ROWS, LANES), lambda: (0, 0))
    return pl.pallas_call(
        _plan_kernel,
        in_specs=[full, full],
        out_specs=[full, full, pl.BlockSpec((8, LANES), lambda: (0, 0))],
        out_shape=[jax.ShapeDtypeStruct((TOK_ROWS, LANES), i32),
                   jax.ShapeDtypeStruct((TOK_ROWS, LANES), i32),
                   jax.ShapeDtypeStruct((8, LANES), i32)],
        name="routing_plan",
    )(e0, e1)


def _dispatch_kernel(d0_ref, d1_ref, x_hbm, zeros_hbm, xs_hbm, sem):
    del zeros_hbm
    n_chunks = N_TOK // DISPATCH_CHUNK

    def wait_chunk():
        pltpu.make_async_copy(x_hbm.at[pl.ds(0, 2 * DISPATCH_CHUNK)],
                              xs_hbm.at[pl.ds(0, 2 * DISPATCH_CHUNK)], sem).wait()

    def chunk(ci, carry):
        def row(r, c2):
            t = ci * DISPATCH_CHUNK + r
            src = x_hbm.at[pl.ds(t, 1)]
            pltpu.make_async_copy(src, xs_hbm.at[pl.ds(d0_ref[t], 1)], sem).start()
            pltpu.make_async_copy(src, xs_hbm.at[pl.ds(d1_ref[t], 1)], sem).start()
            return c2

        lax.fori_loop(0, DISPATCH_CHUNK, row, 0)

        @pl.when(ci > 0)
        def _():
            wait_chunk()

        return carry

    lax.fori_loop(0, n_chunks, chunk, 0)
    wait_chunk()


def dispatch(d0, d1, x1, zeros):
    smem = pl.BlockSpec(memory_space=pltpu.SMEM)
    hbm = pl.BlockSpec(memory_space=pl.ANY)
    return pl.pallas_call(
        _dispatch_kernel,
        in_specs=[smem, smem, hbm, hbm],
        out_specs=hbm,
        out_shape=jax.ShapeDtypeStruct((SORTED_ROWS, D_MODEL), f32),
        scratch_shapes=[pltpu.SemaphoreType.DMA(())],
        input_output_aliases={3: 0},
        name="moe_dispatch",
    )(d0, d1, x1, zeros)


def _expert_kernel(te_ref, nv_ref, xs_ref, wg_ref, wu_ref, wd_ref, o_ref,
                   wg_bf, wu_bf, wd_bf):
    i = pl.program_id(0)

    @pl.when(i < nv_ref[0])
    def _():
        prev = te_ref[jnp.maximum(i - 1, 0)]

        @pl.when((i == 0) | (te_ref[i] != prev))
        def _():
            wg_bf[...] = wg_ref[...].astype(bf16)
            wu_bf[...] = wu_ref[...].astype(bf16)
            wd_bf[...] = wd_ref[...].astype(bf16)

        x = xs_ref[...].astype(bf16)
        gate = jnp.dot(x, wg_bf[...], preferred_element_type=f32)
        up = jnp.dot(x, wu_bf[...], preferred_element_type=f32)
        h = jax.nn.silu(gate) * up
        o_ref[...] = jnp.dot(h.astype(bf16), wd_bf[...], preferred_element_type=f32)

    @pl.when(i >= nv_ref[0])
    def _():
        o_ref[...] = jnp.zeros_like(o_ref)


def expert_mlp(tile_expert, n_valid, xs, w_gate, w_up, w_down, layer):
    base = layer * N_EXPERTS

    def row_map(i, te, nv):
        return (jnp.minimum(i, nv[0] - 1), 0)

    def w_map(i, te, nv):
        return (base + te[i], 0, 0)

    grid_spec = pltpu.PrefetchScalarGridSpec(
        num_scalar_prefetch=2,
        grid=(MAX_TILES,),
        in_specs=[
            pl.BlockSpec((EXP_TM, D_MODEL), row_map),
            pl.BlockSpec((None, D_MODEL, D_EXPERT), w_map),
            pl.BlockSpec((None, D_MODEL, D_EXPERT), w_map),
            pl.BlockSpec((None, D_EXPERT, D_MODEL), w_map),
        ],
        out_specs=pl.BlockSpec((EXP_TM, D_MODEL), lambda i, te, nv: (i, 0)),
        scratch_shapes=[pltpu.VMEM((D_MODEL, D_EXPERT), bf16),
                        pltpu.VMEM((D_MODEL, D_EXPERT), bf16),
                        pltpu.VMEM((D_EXPERT, D_MODEL), bf16)],
    )
    return pl.pallas_call(
        _expert_kernel,
        grid_spec=grid_spec,
        out_shape=jax.ShapeDtypeStruct((SORTED_ROWS, D_MODEL), f32),
        compiler_params=_params("arbitrary"),
        name="moe_experts",
    )(tile_expert, n_valid, xs, w_gate, w_up, w_down)


def _combine_kernel(d0_ref, d1_ref, ys_hbm, x1_ref, rf_ref, g_ref, b_ref, o_ref, buf, sem):
    s = pl.program_id(0)
    n_steps = pl.num_programs(0)
    slot = s % 2

    def issue(step, slot_):
        def row(r, c):
            t = step * LN_TM + r
            pltpu.make_async_copy(ys_hbm.at[pl.ds(d0_ref[t], 1)],
                                  buf.at[slot_, 0, pl.ds(r, 1)], sem.at[slot_]).start()
            pltpu.make_async_copy(ys_hbm.at[pl.ds(d1_ref[t], 1)],
                                  buf.at[slot_, 1, pl.ds(r, 1)], sem.at[slot_]).start()
            return c

        lax.fori_loop(0, LN_TM, row, 0)

    @pl.when(s == 0)
    def _():
        issue(0, 0)

    @pl.when(s + 1 < n_steps)
    def _():
        issue(s + 1, 1 - slot)

    for k in range(TOP_K):
        pltpu.make_async_copy(ys_hbm.at[pl.ds(0, LN_TM)], buf.at[slot, k], sem.at[slot]).wait()

    rf = rf_ref[...]
    y = rf[:, 0:1] * buf[slot, 0] + rf[:, 1:2] * buf[slot, 1]
    y = DEEPNORM_ALPHA * x1_ref[...] + y
    mu = jnp.mean(y, axis=-1, keepdims=True)
    var = jnp.mean(jnp.square(y - mu), axis=-1, keepdims=True)
    o_ref[...] = (y - mu) * lax.rsqrt(var + LN_EPS) * g_ref[...] + b_ref[...]


def combine_ln(d0, d1, ys, x1, route_f, ln_g, ln_b, layer):
    smem = pl.BlockSpec(memory_space=pltpu.SMEM)
    row = pl.BlockSpec((LN_TM, D_MODEL), lambda i: (i, 0))
    vec = pl.BlockSpec((None, 1, D_MODEL), lambda i: (layer, 0, 0))
    return pl.pallas_call(
        _combine_kernel,
        grid=(N_TOK // LN_TM,),
        in_specs=[smem, smem, pl.BlockSpec(memory_space=pl.ANY), row,
                  pl.BlockSpec((LN_TM, LANES), lambda i: (i, 0)), vec, vec],
        out_specs=row,
        out_shape=jax.ShapeDtypeStruct((N_TOK, D_MODEL), f32),
        scratch_shapes=[pltpu.VMEM((2, TOP_K, LN_TM, D_MODEL), f32),
                        pltpu.SemaphoreType.DMA((2,))],
        compiler_params=_params("arbitrary"),
        name="moe_combine_ln",
    )(d0, d1, ys, x1, route_f, ln_g, ln_b)


def kernel(x, sb_w_qkv, sb_w_o, gm_w_in, gm_v_ln_g, gm_v_ln_b, gm_w_spatial, gm_b_spatial, gm_w_out, mix_ln_g, mix_ln_b, moe_w_group_router, moe_b_group_router, moe_w_expert_router, moe_b_expert_router, moe_w_gate, moe_w_up, moe_w_down, ffn_ln_g, ffn_ln_b):
    x = x.reshape(N_TOK, D_MODEL)

    vec3 = lambda a: a.reshape(a.shape[0], 1, a.shape[1])
    gm_g3, gm_b3 = vec3(gm_v_ln_g), vec3(gm_v_ln_b)
    gm_bs_t = jnp.swapaxes(gm_b_spatial, 1, 2)
    mix_g3, mix_b3 = vec3(mix_ln_g), vec3(mix_ln_b)
    ffn_g3, ffn_b3 = vec3(ffn_ln_g), vec3(ffn_ln_b)
    w_er = jnp.transpose(moe_w_expert_router, (0, 2, 1, 3)).reshape(DEPTH, D_MODEL, N_EXPERTS)
    w_route = jnp.concatenate([moe_w_group_router, w_er], axis=-1)
    w_route = jnp.pad(w_route, ((0, 0), (0, 0), (0, LANES - N_ROUTE_COLS)))
    b_route = jnp.concatenate([moe_b_group_router, moe_b_expert_router.reshape(DEPTH, N_EXPERTS)], axis=-1)
    b_route = jnp.pad(b_route, ((0, 0), (0, LANES - N_ROUTE_COLS))).reshape(DEPTH, 1, LANES)
    w_gate = moe_w_gate.reshape(DEPTH * N_EXPERTS, D_MODEL, D_EXPERT)
    w_up = moe_w_up.reshape(DEPTH * N_EXPERTS, D_MODEL, D_EXPERT)
    w_down = moe_w_down.reshape(DEPTH * N_EXPERTS, D_EXPERT, D_MODEL)

    r = lax.broadcasted_iota(i32, (ATT_TQ, ATT_TQ), 0)
    c = lax.broadcasted_iota(i32, (ATT_TQ, ATT_TQ), 1)
    tri = (r > c).astype(bf16)
    zeros_sorted = jnp.zeros((SORTED_ROWS, D_MODEL), f32)

    for i in range(DEPTH):
        j = i // N_MIXERS
        if i % N_MIXERS == 0:
            qkv = matmul_wcast(x, sb_w_qkv, j, tm=512, tn=1024, out_dtype=bf16, name="sb_qkv")
            o = stick_breaking_attention(qkv, tri)
            h = matmul_wcast(o, sb_w_o, j, tm=512, tn=1024, out_dtype=f32, name="sb_out")
        else:
            z = matmul_wcast(x, gm_w_in, j, tm=512, tn=1024, out_dtype=f32, act="gelu", name="gm_in")
            gated = spatial_gate(z, gm_g3, gm_b3, gm_w_spatial, gm_bs_t, j)
            h = matmul_wcast(gated, gm_w_out, j, tm=256, tn=512, out_dtype=f32, name="gm_out")
        x1, route_f, route_i = ln_router(x, h, mix_g3, mix_b3, w_route, b_route, i)
        e0 = route_i[:, 0].reshape(TOK_ROWS, LANES)
        e1 = route_i[:, 1].reshape(TOK_ROWS, LANES)
        d0, d1, info = routing_plan(e0, e1)
        d0 = d0.reshape(N_TOK)
        d1 = d1.reshape(N_TOK)
        xs = dispatch(d0, d1, x1, zeros_sorted)
        ys = expert_mlp(info[0], info[1, :1], xs, w_gate, w_up, w_down, i)
        x = combine_ln(d0, d1, ys, x1, route_f, ffn_g3, ffn_b3, i)
    return x.reshape(BATCH, SEQ, D_MODEL)
```

```python
import functools

import jax
import jax.numpy as jnp
from jax import lax
from jax.experimental import pallas as pl
from jax.experimental.pallas import tpu as pltpu

D_MODEL = 2048
BATCH = 2
SEQ = 4096
DEPTH = 4
N_TOK = BATCH * SEQ
N_MIXERS = 2
SB_HEADS = 16
HEAD_DIM = D_MODEL // SB_HEADS
GM_CHUNK = 128
GM_FFN = 6 * D_MODEL
GM_HALF = GM_FFN // 2
GM_GROUPS = 16
GM_GROUP_DIM = GM_HALF // GM_GROUPS
N_GROUPS = 4
EXPERTS_PER_GROUP = 8
N_EXPERTS = N_GROUPS * EXPERTS_PER_GROUP
TOP_K = 2
D_EXPERT = 512
LN_EPS = 1e-5
DEEPNORM_ALPHA = (2.0 * DEPTH) ** 0.25

LANES = 128
VMEM_LIMIT = 56 * 1024 * 1024

ATT_TQ = 256
ATT_HEADS_PER_STEP = 4
LN_TM = 256
EXP_TM = 256
N_ROUTE_COLS = N_GROUPS + N_EXPERTS
MAX_TILES = (N_TOK * TOP_K) // EXP_TM + N_EXPERTS
SORTED_ROWS = MAX_TILES * EXP_TM
TOK_ROWS = N_TOK // LANES

f32 = jnp.float32
bf16 = jnp.bfloat16
i32 = jnp.int32


def _params(*sem):
    return pltpu.CompilerParams(dimension_semantics=sem, vmem_limit_bytes=VMEM_LIMIT)


def _mm_kernel(x_ref, w_ref, o_ref, wbf_ref, *, act):
    @pl.when(pl.program_id(1) == 0)
    def _():
        wbf_ref[...] = w_ref[...].astype(bf16)

    acc = jnp.dot(x_ref[...].astype(bf16), wbf_ref[...], preferred_element_type=f32)
    if act == "gelu":
        acc = jax.nn.gelu(acc)
    o_ref[...] = acc.astype(o_ref.dtype)


def matmul_wcast(x, w_stack, layer, *, tm, tn, out_dtype, act=None, name):
    m, k = x.shape
    n = w_stack.shape[-1]
    return pl.pallas_call(
        functools.partial(_mm_kernel, act=act),
        grid=(n // tn, m // tm),
        in_specs=[
            pl.BlockSpec((tm, k), lambda j, i: (i, 0)),
            pl.BlockSpec((None, k, tn), lambda j, i: (layer, 0, j)),
        ],
        out_specs=pl.BlockSpec((tm, tn), lambda j, i: (i, j)),
        out_shape=jax.ShapeDtypeStruct((m, n), out_dtype),
        scratch_shapes=[pltpu.VMEM((k, tn), bf16)],
        compiler_params=_params("arbitrary", "arbitrary"),
        name=name,
    )(x, w_stack)


def _attn_kernel(q_ref, k_ref, v_ref, m_ref, o_ref, acc_ref, carry_ref):
    qi = pl.program_id(2)
    scale = HEAD_DIM ** -0.5

    def tile(j, diagonal):
        start = pl.multiple_of(j * ATT_TQ, ATT_TQ)
        m = m_ref[...]
        if diagonal:
            row = lax.broadcasted_iota(i32, (ATT_TQ, ATT_TQ), 0)
            col = lax.broadcasted_iota(i32, (ATT_TQ, ATT_TQ), 1)
            causal = col < row
        heads = range(ATT_HEADS_PER_STEP)
        hs = [slice(hh * HEAD_DIM, (hh + 1) * HEAD_DIM) for hh in heads]
        log_beta, log_rest_term, rest, att = {}, {}, {}, {}
        for hh in heads:
            k = k_ref[pl.ds(start, ATT_TQ), hs[hh]]
            z = lax.dot_general(q_ref[:, hs[hh]], k, (((1,), (1,)), ((), ())),
                                preferred_element_type=f32) * scale
            log_beta[hh] = jnp.minimum(z, 0.0) - jnp.log(1.0 + jnp.exp(-jnp.abs(z)))
            lrt = log_beta[hh] - z
            log_rest_term[hh] = jnp.where(causal, lrt, 0.0) if diagonal else lrt
        for hh in heads:
            hi = log_rest_term[hh].astype(bf16)
            lo = (log_rest_term[hh] - hi.astype(f32)).astype(bf16)
            rest[hh] = (jnp.dot(hi, m, preferred_element_type=f32)
                        + jnp.dot(lo, m, preferred_element_type=f32))
            if diagonal:
                att[hh] = jnp.where(causal, jnp.exp(log_beta[hh] + rest[hh]), 0.0)
            else:
                att[hh] = jnp.exp(log_beta[hh] + rest[hh] + carry_ref[hh])
        for hh in heads:
            v = v_ref[pl.ds(start, ATT_TQ), hs[hh]]
            pv = jnp.dot(att[hh].astype(bf16), v, preferred_element_type=f32)
            block_total = rest[hh][:, 0:1] + log_rest_term[hh][:, 0:1]
            if diagonal:
                acc_ref[:, hs[hh]] = pv
                carry_ref[hh] = block_total
            else:
                acc_ref[:, hs[hh]] += pv
                carry_ref[hh] += block_total

    tile(qi, True)

    def body(i, c):
        tile(qi - 1 - i, False)
        return c

    lax.fori_loop(0, qi, body, 0)
    o_ref[...] = acc_ref[...].astype(o_ref.dtype)


def stick_breaking_attention(qkv, tri):
    nq = SEQ // ATT_TQ
    width = ATT_HEADS_PER_STEP * HEAD_DIM
    n_hb = SB_HEADS // ATT_HEADS_PER_STEP
    return pl.pallas_call(
        _attn_kernel,
        grid=(BATCH, n_hb, nq),
        in_specs=[
            pl.BlockSpec((ATT_TQ, width), lambda b, h, i: (b * nq + i, h)),
            pl.BlockSpec((SEQ, width), lambda b, h, i: (b, n_hb + h)),
            pl.BlockSpec((SEQ, width), lambda b, h, i: (b, 2 * n_hb + h)),
            pl.BlockSpec((ATT_TQ, ATT_TQ), lambda b, h, i: (0, 0)),
        ],
        out_specs=pl.BlockSpec((ATT_TQ, width), lambda b, h, i: (b * nq + i, h)),
        out_shape=jax.ShapeDtypeStruct((N_TOK, D_MODEL), bf16),
        scratch_shapes=[pltpu.VMEM((ATT_TQ, width), f32),
                        pltpu.VMEM((ATT_HEADS_PER_STEP, ATT_TQ, 1), f32)],
        compiler_params=_params("arbitrary", "arbitrary", "arbitrary"),
        name="sb_attention",
    )(qkv, qkv, qkv, tri)


def _gate_kernel(u_ref, v_ref, g_ref, b_ref, ws_ref, bs_ref, o_ref):
    v = v_ref[...]
    mu = jnp.mean(v, axis=-1, keepdims=True)
    var = jnp.mean(jnp.square(v - mu), axis=-1, keepdims=True)
    rstd = lax.rsqrt(var + LN_EPS)
    row = lax.broadcasted_iota(i32, (GM_CHUNK, GM_CHUNK), 0)
    col = lax.broadcasted_iota(i32, (GM_CHUNK, GM_CHUNK), 1)
    lower = col <= row
    for g in range(GM_GROUPS):
        sl = slice(g * GM_GROUP_DIM, (g + 1) * GM_GROUP_DIM)
        vn = (v_ref[:, sl] - mu) * rstd * g_ref[:, sl] + b_ref[:, sl]
        w = jnp.where(lower, ws_ref[g], 0.0).astype(bf16)
        s = jnp.dot(w, vn.astype(bf16), preferred_element_type=f32) + bs_ref[:, g:g + 1]
        o_ref[:, sl] = (u_ref[:, sl] * s).astype(o_ref.dtype)


def spatial_gate(z, ln_g, ln_b, w_spatial, b_spatial_t, layer):
    n_chunks = N_TOK // GM_CHUNK
    return pl.pallas_call(
        _gate_kernel,
        grid=(n_chunks,),
        in_specs=[
            pl.BlockSpec((GM_CHUNK, GM_HALF), lambda c: (c, 0)),
            pl.BlockSpec((GM_CHUNK, GM_HALF), lambda c: (c, 1)),
            pl.BlockSpec((None, 1, GM_HALF), lambda c: (layer, 0, 0)),
            pl.BlockSpec((None, 1, GM_HALF), lambda c: (layer, 0, 0)),
            pl.BlockSpec((None, GM_GROUPS, GM_CHUNK, GM_CHUNK), lambda c: (layer, 0, 0, 0)),
            pl.BlockSpec((None, GM_CHUNK, GM_GROUPS), lambda c: (layer, 0, 0)),
        ],
        out_specs=pl.BlockSpec((GM_CHUNK, GM_HALF), lambda c: (c, 0)),
        out_shape=jax.ShapeDtypeStruct((N_TOK, GM_HALF), bf16),
        compiler_params=_params("arbitrary"),
        name="gm_spatial_gate",
    )(z, z, ln_g, ln_b, w_spatial, b_spatial_t)


def _split_bf16(a):
    hi = a.astype(bf16)
    lo = (a - hi.astype(f32)).astype(bf16)
    return hi, lo


def _ln_router_kernel(x_ref, h_ref, g_ref, b_ref, wr_ref, br_ref,
                      x1_ref, rf_ref, ri_ref, wh_ref, wl_ref):
    @pl.when(pl.program_id(0) == 0)
    def _():
        hi, lo = _split_bf16(wr_ref[...])
        wh_ref[...] = hi
        wl_ref[...] = lo

    y = DEEPNORM_ALPHA * x_ref[...] + h_ref[...]
    mu = jnp.mean(y, axis=-1, keepdims=True)
    var = jnp.mean(jnp.square(y - mu), axis=-1, keepdims=True)
    x1 = (y - mu) * lax.rsqrt(var + LN_EPS) * g_ref[...] + b_ref[...]
    x1_ref[...] = x1

    xh, xl = _split_bf16(x1)
    wh = wh_ref[...]
    logits = (jnp.dot(xh, wh, preferred_element_type=f32)
              + jnp.dot(xl, wh, preferred_element_type=f32)
              + jnp.dot(xh, wl_ref[...], preferred_element_type=f32)) + br_ref[...]

    lane = lax.broadcasted_iota(i32, logits.shape, 1)
    lane_f = lane.astype(f32)
    neg_inf = jnp.float32(-jnp.inf)
    big = jnp.float32(1e9)

    def masked_argmax(mask):
        vals = jnp.where(mask, logits, neg_inf)
        best = jnp.max(vals, axis=-1, keepdims=True)
        idx = jnp.min(jnp.where(mask & (logits == best), lane_f, big), axis=-1, keepdims=True)
        return best, idx

    gmask = lane < N_GROUPS
    gmax, gidx = masked_argmax(gmask)
    denom = jnp.sum(jnp.where(gmask, jnp.exp(logits - gmax), 0.0), axis=-1, keepdims=True)
    p_group = 1.0 / denom

    first = N_GROUPS + EXPERTS_PER_GROUP * gidx
    emask = (lane_f >= first) & (lane_f < first + EXPERTS_PER_GROUP)
    v1, i1 = masked_argmax(emask)
    v2, i2 = masked_argmax(emask & (lane_f != i1))
    t = jnp.exp(v2 - v1)
    w0 = p_group * (1.0 / (1.0 + t))
    w1 = p_group * (t / (1.0 + t))

    rf_ref[...] = jnp.where(lane == 0, w0, jnp.where(lane == 1, w1, 0.0))
    e0 = (i1 - N_GROUPS).astype(i32)
    e1 = (i2 - N_GROUPS).astype(i32)
    ri_ref[...] = jnp.where(lane == 0, e0, jnp.where(lane == 1, e1, 0))


def ln_router(x, h, ln_g, ln_b, w_route, b_route, layer):
    row = pl.BlockSpec((LN_TM, D_MODEL), lambda i: (i, 0))
    vec = pl.BlockSpec((None, 1, D_MODEL), lambda i: (layer, 0, 0))
    small = pl.BlockSpec((LN_TM, LANES), lambda i: (i, 0))
    return pl.pallas_call(
        _ln_router_kernel,
        grid=(N_TOK // LN_TM,),
        in_specs=[row, row, vec, vec,
                  pl.BlockSpec((None, D_MODEL, LANES), lambda i: (layer, 0, 0)),
                  pl.BlockSpec((None, 1, LANES), lambda i: (layer, 0, 0))],
        out_specs=[row, small, small],
        out_shape=[jax.ShapeDtypeStruct((N_TOK, D_MODEL), f32),
                   jax.ShapeDtypeStruct((N_TOK, LANES), f32),
                   jax.ShapeDtypeStruct((N_TOK, LANES), i32)],
        scratch_shapes=[pltpu.VMEM((D_MODEL, LANES), bf16), pltpu.VMEM((D_MODEL, LANES), bf16)],
        compiler_params=_params("arbitrary"),
        name="ln_router",
    )(x, h, ln_g, ln_b, w_route, b_route)


def _plan_kernel(e0_ref, e1_ref, d0_ref, d1_ref, info_ref):
    e0 = e0_ref[...]
    e1 = e1_ref[...]
    r = lax.broadcasted_iota(i32, (LANES, LANES), 0)
    c = lax.broadcasted_iota(i32, (LANES, LANES), 1)
    incl = (r <= c).astype(bf16)
    ones_l = jnp.ones((LANES, LANES), bf16)
    rr = lax.broadcasted_iota(i32, (TOK_ROWS, TOK_ROWS), 0)
    cc = lax.broadcasted_iota(i32, (TOK_ROWS, TOK_ROWS), 1)
    before = (cc < rr).astype(bf16)
    ones_r = jnp.ones((TOK_ROWS, TOK_ROWS), bf16)
    tile_start = (lax.broadcasted_iota(i32, (8, LANES), 1) * EXP_TM).astype(f32)

    off = jnp.zeros((TOK_ROWS, LANES), f32)
    d0 = jnp.zeros((TOK_ROWS, LANES), f32)
    d1 = jnp.zeros((TOK_ROWS, LANES), f32)
    tile_expert = jnp.zeros((8, LANES), f32)
    seg_end = jnp.zeros((8, LANES), f32)
    lane8 = lax.broadcasted_iota(i32, (8, LANES), 1)
    for e in range(N_EXPERTS):
        m0 = e0 == e
        m1 = e1 == e
        m = (m0 | m1).astype(f32).astype(bf16)
        within = jnp.dot(m, incl, preferred_element_type=f32)
        row_sum = jnp.dot(m, ones_l, preferred_element_type=f32).astype(bf16)
        above = jnp.dot(before, row_sum, preferred_element_type=f32)
        total = jnp.dot(ones_r, row_sum, preferred_element_type=f32)
        pos = off + above + within - 1.0
        d0 = jnp.where(m0, pos, d0)
        d1 = jnp.where(m1, pos, d1)
        padded = jnp.floor((total + (EXP_TM - 1)) * (1.0 / EXP_TM)) * EXP_TM
        off = off + padded
        tile_expert = tile_expert + (tile_start >= off[0:8, :]).astype(f32)
        seg_end = jnp.where(lane8 == e, off[0:8, :] * (1.0 / EXP_TM), seg_end)
    d0_ref[...] = d0.astype(i32)
    d1_ref[...] = d1.astype(i32)
    n_tiles = off[0:8, :] * (1.0 / EXP_TM)
    sub = lax.broadcasted_iota(i32, (8, LANES), 0)
    info = jnp.where(sub == 0, jnp.minimum(tile_expert, N_EXPERTS - 1.0),
                     jnp.where(sub == 1, n_tiles, seg_end))
    info_ref[...] = info.astype(i32)


def routing_plan(e0, e1):
    full = pl.BlockSpec((TOK_ROWS, LANES), lambda: (0, 0))
    return pl.pallas_call(
        _plan_kernel,
        in_specs=[full, full],
        out_specs=[full, full, pl.BlockSpec((8, LANES), lambda: (0, 0))],
        out_shape=[jax.ShapeDtypeStruct((TOK_ROWS, LANES), i32),
                   jax.ShapeDtypeStruct((TOK_ROWS, LANES), i32),
                   jax.ShapeDtypeStruct((8, LANES), i32)],
        name="routing_plan",
    )(e0, e1)


def _dispatch_kernel(d0_ref, d1_ref, x_ref, zeros_hbm, xs_hbm, sem):
    del zeros_hbm
    base = pl.program_id(0) * LN_TM

    def row(r, c):
        t = base + r
        src = x_ref.at[pl.ds(r, 1)]
        pltpu.make_async_copy(src, xs_hbm.at[pl.ds(d0_ref[t], 1)], sem).start()
        pltpu.make_async_copy(src, xs_hbm.at[pl.ds(d1_ref[t], 1)], sem).start()
        return c

    lax.fori_loop(0, LN_TM, row, 0, unroll=8)
    for _ in range(TOP_K):
        pltpu.make_async_copy(x_ref, xs_hbm.at[pl.ds(0, LN_TM)], sem).wait()


def dispatch(d0, d1, x1, zeros):
    smem = pl.BlockSpec(memory_space=pltpu.SMEM)
    hbm = pl.BlockSpec(memory_space=pl.ANY)
    return pl.pallas_call(
        _dispatch_kernel,
        grid=(N_TOK // LN_TM,),
        in_specs=[smem, smem, pl.BlockSpec((LN_TM, D_MODEL), lambda i: (i, 0)), hbm],
        out_specs=hbm,
        out_shape=jax.ShapeDtypeStruct((SORTED_ROWS, D_MODEL), f32),
        scratch_shapes=[pltpu.SemaphoreType.DMA(())],
        input_output_aliases={3: 0},
        compiler_params=_params("arbitrary"),
        name="moe_dispatch",
    )(d0, d1, x1, zeros)


def _expert_kernel(te_ref, nv_ref, seg_end_ref, xs_ref, wg_hbm, wu_hbm, wd_hbm, o_ref,
                   wg_f, wu_f, wd_f, wg_bf, wu_bf, wd_bf, seg_ref, sem, *, base):
    i = pl.program_id(0)
    n_valid = nv_ref[0]

    def weight_copies(e, slot):
        return [pltpu.make_async_copy(src.at[base + e], dst.at[slot], sem.at[slot, k])
                for k, (src, dst) in enumerate(((wg_hbm, wg_f), (wu_hbm, wu_f), (wd_hbm, wd_f)))]

    @pl.when(i < n_valid)
    def _():
        e = te_ref[i]
        prev = te_ref[jnp.maximum(i - 1, 0)]

        @pl.when(i == 0)
        def _():
            seg_ref[0] = 0
            for cp in weight_copies(e, 0):
                cp.start()

        @pl.when((i == 0) | (e != prev))
        def _():
            @pl.when(i > 0)
            def _():
                seg_ref[0] = seg_ref[0] + 1

            slot = seg_ref[0] % 2
            for cp in weight_copies(e, slot):
                cp.wait()
            nxt = seg_end_ref[e]

            @pl.when(nxt < n_valid)
            def _():
                for cp in weight_copies(te_ref[nxt], 1 - slot):
                    cp.start()

            wg_bf[...] = wg_f[slot].astype(bf16)
            wu_bf[...] = wu_f[slot].astype(bf16)
            wd_bf[...] = wd_f[slot].astype(bf16)

        x = xs_ref[...].astype(bf16)
        gate = jnp.dot(x, wg_bf[...], preferred_element_type=f32)
        up = jnp.dot(x, wu_bf[...], preferred_element_type=f32)
        h = jax.nn.silu(gate) * up
        o_ref[...] = jnp.dot(h.astype(bf16), wd_bf[...], preferred_element_type=f32)

    @pl.when(i >= nv_ref[0])
    def _():
        o_ref[...] = jnp.zeros_like(o_ref)


def expert_mlp(tile_expert, n_valid, seg_end, xs, w_gate, w_up, w_down, layer):
    def row_map(i, te, nv, se):
        return (jnp.minimum(i, nv[0] - 1), 0)

    hbm = pl.BlockSpec(memory_space=pl.ANY)
    grid_spec = pltpu.PrefetchScalarGridSpec(
        num_scalar_prefetch=3,
        grid=(MAX_TILES,),
        in_specs=[pl.BlockSpec((EXP_TM, D_MODEL), row_map), hbm, hbm, hbm],
        out_specs=pl.BlockSpec((EXP_TM, D_MODEL), lambda i, te, nv, se: (i, 0)),
        scratch_shapes=[pltpu.VMEM((2, D_MODEL, D_EXPERT), f32),
                        pltpu.VMEM((2, D_MODEL, D_EXPERT), f32),
                        pltpu.VMEM((2, D_EXPERT, D_MODEL), f32),
                        pltpu.VMEM((D_MODEL, D_EXPERT), bf16),
                        pltpu.VMEM((D_MODEL, D_EXPERT), bf16),
                        pltpu.VMEM((D_EXPERT, D_MODEL), bf16),
                        pltpu.SMEM((1,), i32),
                        pltpu.SemaphoreType.DMA((2, 3))],
    )
    return pl.pallas_call(
        functools.partial(_expert_kernel, base=layer * N_EXPERTS),
        grid_spec=grid_spec,
        out_shape=jax.ShapeDtypeStruct((SORTED_ROWS, D_MODEL), f32),
        compiler_params=_params("arbitrary"),
        name="moe_experts",
    )(tile_expert, n_valid, seg_end, xs, w_gate, w_up, w_down)


def _combine_kernel(d0_ref, d1_ref, ys_hbm, x1_ref, rf_ref, g_ref, b_ref, o_ref, buf, sem):
    s = pl.program_id(0)
    n_steps = pl.num_programs(0)
    slot = s % 2

    def issue(step, slot_):
        def row(r, c):
            t = step * LN_TM + r
            pltpu.make_async_copy(ys_hbm.at[pl.ds(d0_ref[t], 1)],
                                  buf.at[slot_, 0, pl.ds(r, 1)], sem.at[slot_]).start()
            pltpu.make_async_copy(ys_hbm.at[pl.ds(d1_ref[t], 1)],
                                  buf.at[slot_, 1, pl.ds(r, 1)], sem.at[slot_]).start()
            return c

        lax.fori_loop(0, LN_TM, row, 0)

    @pl.when(s == 0)
    def _():
        issue(0, 0)

    @pl.when(s + 1 < n_steps)
    def _():
        issue(s + 1, 1 - slot)

    for k in range(TOP_K):
        pltpu.make_async_copy(ys_hbm.at[pl.ds(0, LN_TM)], buf.at[slot, k], sem.at[slot]).wait()

    rf = rf_ref[...]
    y = rf[:, 0:1] * buf[slot, 0] + rf[:, 1:2] * buf[slot, 1]
    y = DEEPNORM_ALPHA * x1_ref[...] + y
    mu = jnp.mean(y, axis=-1, keepdims=True)
    var = jnp.mean(jnp.square(y - mu), axis=-1, keepdims=True)
    o_ref[...] = (y - mu) * lax.rsqrt(var + LN_EPS) * g_ref[...] + b_ref[...]


def combine_ln(d0, d1, ys, x1, route_f, ln_g, ln_b, layer):
    smem = pl.BlockSpec(memory_space=pltpu.SMEM)
    row = pl.BlockSpec((LN_TM, D_MODEL), lambda i: (i, 0))
    vec = pl.BlockSpec((None, 1, D_MODEL), lambda i: (layer, 0, 0))
    return pl.pallas_call(
        _combine_kernel,
        grid=(N_TOK // LN_TM,),
        in_specs=[smem, smem, pl.BlockSpec(memory_space=pl.ANY), row,
                  pl.BlockSpec((LN_TM, LANES), lambda i: (i, 0)), vec, vec],
        out_specs=row,
        out_shape=jax.ShapeDtypeStruct((N_TOK, D_MODEL), f32),
        scratch_shapes=[pltpu.VMEM((2, TOP_K, LN_TM, D_MODEL), f32),
                        pltpu.SemaphoreType.DMA((2,))],
        compiler_params=_params("arbitrary"),
        name="moe_combine_ln",
    )(d0, d1, ys, x1, route_f, ln_g, ln_b)


def kernel(x, sb_w_qkv, sb_w_o, gm_w_in, gm_v_ln_g, gm_v_ln_b, gm_w_spatial, gm_b_spatial, gm_w_out, mix_ln_g, mix_ln_b, moe_w_group_router, moe_b_group_router, moe_w_expert_router, moe_b_expert_router, moe_w_gate, moe_w_up, moe_w_down, ffn_ln_g, ffn_ln_b):
    x = x.reshape(N_TOK, D_MODEL)

    vec3 = lambda a: a.reshape(a.shape[0], 1, a.shape[1])
    gm_g3, gm_b3 = vec3(gm_v_ln_g), vec3(gm_v_ln_b)
    gm_bs_t = jnp.swapaxes(gm_b_spatial, 1, 2)
    mix_g3, mix_b3 = vec3(mix_ln_g), vec3(mix_ln_b)
    ffn_g3, ffn_b3 = vec3(ffn_ln_g), vec3(ffn_ln_b)
    w_er = jnp.transpose(moe_w_expert_router, (0, 2, 1, 3)).reshape(DEPTH, D_MODEL, N_EXPERTS)
    w_route = jnp.concatenate([moe_w_group_router, w_er], axis=-1)
    w_route = jnp.pad(w_route, ((0, 0), (0, 0), (0, LANES - N_ROUTE_COLS)))
    b_route = jnp.concatenate([moe_b_group_router, moe_b_expert_router.reshape(DEPTH, N_EXPERTS)], axis=-1)
    b_route = jnp.pad(b_route, ((0, 0), (0, LANES - N_ROUTE_COLS))).reshape(DEPTH, 1, LANES)
    w_gate = moe_w_gate.reshape(DEPTH * N_EXPERTS, D_MODEL, D_EXPERT)
    w_up = moe_w_up.reshape(DEPTH * N_EXPERTS, D_MODEL, D_EXPERT)
    w_down = moe_w_down.reshape(DEPTH * N_EXPERTS, D_EXPERT, D_MODEL)

    r = lax.broadcasted_iota(i32, (ATT_TQ, ATT_TQ), 0)
    c = lax.broadcasted_iota(i32, (ATT_TQ, ATT_TQ), 1)
    tri = (r > c).astype(bf16)
    zeros_sorted = jnp.zeros((SORTED_ROWS, D_MODEL), f32)

    for i in range(DEPTH):
        j = i // N_MIXERS
        if i % N_MIXERS == 0:
            qkv = matmul_wcast(x, sb_w_qkv, j, tm=512, tn=1024, out_dtype=bf16, name="sb_qkv")
            o = stick_breaking_attention(qkv, tri)
            h = matmul_wcast(o, sb_w_o, j, tm=512, tn=1024, out_dtype=f32, name="sb_out")
        else:
            z = matmul_wcast(x, gm_w_in, j, tm=512, tn=1024, out_dtype=f32, act="gelu", name="gm_in")
            gated = spatial_gate(z, gm_g3, gm_b3, gm_w_spatial, gm_bs_t, j)
            h = matmul_wcast(gated, gm_w_out, j, tm=256, tn=512, out_dtype=f32, name="gm_out")
        x1, route_f, route_i = ln_router(x, h, mix_g3, mix_b3, w_route, b_route, i)
        e0 = route_i[:, 0].reshape(TOK_ROWS, LANES)
        e1 = route_i[:, 1].reshape(TOK_ROWS, LANES)
        d0, d1, info = routing_plan(e0, e1)
        d0 = d0.reshape(N_TOK)
        d1 = d1.reshape(N_TOK)
        xs = dispatch(d0, d1, x1, zeros_sorted)
        ys = expert_mlp(info[0], info[1, :1], info[2], xs, w_gate, w_up, w_down, i)
        x = combine_ln(d0, d1, ys, x1, route_f, ffn_g3, ffn_b3, i)
    return x.reshape(BATCH, SEQ, D_MODEL)
```

```python
import functools

import jax
import jax.numpy as jnp
from jax import lax
from jax.experimental import pallas as pl
from jax.experimental.pallas import tpu as pltpu

D_MODEL = 2048
BATCH = 2
SEQ = 4096
DEPTH = 4
N_TOK = BATCH * SEQ
N_MIXERS = 2
SB_HEADS = 16
HEAD_DIM = D_MODEL // SB_HEADS
GM_CHUNK = 128
GM_FFN = 6 * D_MODEL
GM_HALF = GM_FFN // 2
GM_GROUPS = 16
GM_GROUP_DIM = GM_HALF // GM_GROUPS
N_GROUPS = 4
EXPERTS_PER_GROUP = 8
N_EXPERTS = N_GROUPS * EXPERTS_PER_GROUP
TOP_K = 2
D_EXPERT = 512
LN_EPS = 1e-5
DEEPNORM_ALPHA = (2.0 * DEPTH) ** 0.25
LOG2_E = 1.4426950408889634

LANES = 128
VMEM_LIMIT = 56 * 1024 * 1024

ATT_TK = 256
ATT_QSUB = 2
ATT_HEADS_PER_STEP = 4
ATT_SKEW = 1
LN_TM = 256
EXP_TM = 256
N_ROUTE_COLS = N_GROUPS + N_EXPERTS
MAX_TILES = (N_TOK * TOP_K) // EXP_TM + N_EXPERTS
SORTED_ROWS = MAX_TILES * EXP_TM
TOK_ROWS = N_TOK // LANES

f32 = jnp.float32
bf16 = jnp.bfloat16
i32 = jnp.int32


def _params(*sem):
    return pltpu.CompilerParams(dimension_semantics=sem, vmem_limit_bytes=VMEM_LIMIT)


def _mm_kernel(x_ref, w_ref, o_ref, wbf_ref, *, act):
    @pl.when(pl.program_id(1) == 0)
    def _():
        wbf_ref[...] = w_ref[...].astype(bf16)

    acc = jnp.dot(x_ref[...].astype(bf16), wbf_ref[...], preferred_element_type=f32)
    if act == "gelu":
        acc = jax.nn.gelu(acc)
    o_ref[...] = acc.astype(o_ref.dtype)


def matmul_wcast(x, w_stack, layer, *, tm, tn, out_dtype, act=None, name):
    m, k = x.shape
    n = w_stack.shape[-1]
    return pl.pallas_call(
        functools.partial(_mm_kernel, act=act),
        grid=(n // tn, m // tm),
        in_specs=[
            pl.BlockSpec((tm, k), lambda j, i: (i, 0)),
            pl.BlockSpec((None, k, tn), lambda j, i: (layer, 0, j)),
        ],
        out_specs=pl.BlockSpec((tm, tn), lambda j, i: (i, j)),
        out_shape=jax.ShapeDtypeStruct((m, n), out_dtype),
        scratch_shapes=[pltpu.VMEM((k, tn), bf16)],
        compiler_params=_params("arbitrary", "arbitrary"),
        name=name,
    )(x, w_stack)


def _attn_kernel(q_ref, k_ref, v_ref, m_ref, o_ref, acc_ref, carry_ref):
    qi = pl.program_id(2)
    scale = HEAD_DIM ** -0.5
    row = lax.broadcasted_iota(i32, (ATT_TK, ATT_TK), 0)
    col = lax.broadcasted_iota(i32, (ATT_TK, ATT_TK), 1)
    causal = col < row

    def key_block(j, sub_modes):
        start = pl.multiple_of(j * ATT_TK, ATT_TK)
        m = m_ref[...]
        chains = [(sub, hh) for sub in sorted(sub_modes) for hh in range(ATT_HEADS_PER_STEP)]
        log_beta, log_rest_term, rest, att = {}, {}, {}, {}

        def where(c):
            sub, hh = c
            return (slice(sub * ATT_TK, (sub + 1) * ATT_TK),
                    slice(hh * HEAD_DIM, (hh + 1) * HEAD_DIM))

        def scores(c):
            rs, hs = where(c)
            k = k_ref[pl.ds(start, ATT_TK), hs]
            z = lax.dot_general(q_ref[rs, hs], k, (((1,), (1,)), ((), ())),
                                preferred_element_type=f32) * (scale * LOG2_E)
            log_beta[c] = jnp.minimum(z, 0.0) - jnp.log(1.0 + jnp.exp2(-jnp.abs(z))) * LOG2_E
            lrt = log_beta[c] - z
            log_rest_term[c] = jnp.where(causal, lrt, 0.0) if sub_modes[c[0]] == "diag" else lrt

        def suffix_sum(c):
            rest[c] = jnp.dot(log_rest_term[c].astype(bf16), m, preferred_element_type=f32)
            if sub_modes[c[0]] == "diag":
                att[c] = jnp.where(causal, jnp.exp2(log_beta[c] + rest[c]), 0.0)
            else:
                att[c] = jnp.exp2(log_beta[c] + rest[c] + carry_ref[c[0], c[1]])

        def weighted_values(c):
            rs, hs = where(c)
            v = v_ref[pl.ds(start, ATT_TK), hs]
            pv = jnp.dot(att[c].astype(bf16), v, preferred_element_type=f32)
            block_total = rest[c][:, 0:1] + log_rest_term[c][:, 0:1]
            if sub_modes[c[0]] == "diag":
                acc_ref[rs, hs] = pv
                carry_ref[c[0], c[1]] = block_total
            else:
                acc_ref[rs, hs] += pv
                carry_ref[c[0], c[1]] += block_total

        n = len(chains)
        for step in range(n + 2 * ATT_SKEW):
            if step < n:
                scores(chains[step])
            if 0 <= step - ATT_SKEW < n:
                suffix_sum(chains[step - ATT_SKEW])
            if 0 <= step - 2 * ATT_SKEW < n:
                weighted_values(chains[step - 2 * ATT_SKEW])

    top = ATT_QSUB * qi
    for s in reversed(range(ATT_QSUB)):
        key_block(top + s, {t: ("diag" if t == s else "full") for t in range(s, ATT_QSUB)})

    def body(i, c):
        key_block(top - 1 - i, {t: "full" for t in range(ATT_QSUB)})
        return c

    lax.fori_loop(0, top, body, 0)
    o_ref[...] = acc_ref[...].astype(o_ref.dtype)


def stick_breaking_attention(qkv, tri):
    tq = ATT_QSUB * ATT_TK
    nq = SEQ // tq
    width = ATT_HEADS_PER_STEP * HEAD_DIM
    n_hb = SB_HEADS // ATT_HEADS_PER_STEP
    return pl.pallas_call(
        _attn_kernel,
        grid=(BATCH, n_hb, nq),
        in_specs=[
            pl.BlockSpec((tq, width), lambda b, h, i: (b * nq + i, h)),
            pl.BlockSpec((SEQ, width), lambda b, h, i: (b, n_hb + h)),
            pl.BlockSpec((SEQ, width), lambda b, h, i: (b, 2 * n_hb + h)),
            pl.BlockSpec((ATT_TK, ATT_TK), lambda b, h, i: (0, 0)),
        ],
        out_specs=pl.BlockSpec((tq, width), lambda b, h, i: (b * nq + i, h)),
        out_shape=jax.ShapeDtypeStruct((N_TOK, D_MODEL), bf16),
        scratch_shapes=[pltpu.VMEM((tq, width), f32),
                        pltpu.VMEM((ATT_QSUB, ATT_HEADS_PER_STEP, ATT_TK, 1), f32)],
        compiler_params=_params("arbitrary", "arbitrary", "arbitrary"),
        name="sb_attention",
    )(qkv, qkv, qkv, tri)


def _gate_kernel(u_ref, v_ref, g_ref, b_ref, ws_ref, bs_ref, o_ref):
    v = v_ref[...].astype(f32)
    mu = jnp.mean(v, axis=-1, keepdims=True)
    var = jnp.mean(jnp.square(v - mu), axis=-1, keepdims=True)
    rstd = lax.rsqrt(var + LN_EPS)
    row = lax.broadcasted_iota(i32, (GM_CHUNK, GM_CHUNK), 0)
    col = lax.broadcasted_iota(i32, (GM_CHUNK, GM_CHUNK), 1)
    lower = col <= row
    for g in range(GM_GROUPS):
        sl = slice(g * GM_GROUP_DIM, (g + 1) * GM_GROUP_DIM)
        vn = (v_ref[:, sl].astype(f32) - mu) * rstd * g_ref[:, sl] + b_ref[:, sl]
        w = jnp.where(lower, ws_ref[g], 0.0).astype(bf16)
        s = jnp.dot(w, vn.astype(bf16), preferred_element_type=f32) + bs_ref[:, g:g + 1]
        o_ref[:, sl] = (u_ref[:, sl].astype(f32) * s).astype(o_ref.dtype)


def spatial_gate(z, ln_g, ln_b, w_spatial, b_spatial_t, layer):
    n_chunks = N_TOK // GM_CHUNK
    return pl.pallas_call(
        _gate_kernel,
        grid=(n_chunks,),
        in_specs=[
            pl.BlockSpec((GM_CHUNK, GM_HALF), lambda c: (c, 0)),
            pl.BlockSpec((GM_CHUNK, GM_HALF), lambda c: (c, 1)),
            pl.BlockSpec((None, 1, GM_HALF), lambda c: (layer, 0, 0)),
            pl.BlockSpec((None, 1, GM_HALF), lambda c: (layer, 0, 0)),
            pl.BlockSpec((None, GM_GROUPS, GM_CHUNK, GM_CHUNK), lambda c: (layer, 0, 0, 0)),
            pl.BlockSpec((None, GM_CHUNK, GM_GROUPS), lambda c: (layer, 0, 0)),
        ],
        out_specs=pl.BlockSpec((GM_CHUNK, GM_HALF), lambda c: (c, 0)),
        out_shape=jax.ShapeDtypeStruct((N_TOK, GM_HALF), bf16),
        compiler_params=_params("arbitrary"),
        name="gm_spatial_gate",
    )(z, z, ln_g, ln_b, w_spatial, b_spatial_t)


def _split_bf16(a):
    hi = a.astype(bf16)
    lo = (a - hi.astype(f32)).astype(bf16)
    return hi, lo


def _ln_router_kernel(x_ref, h_ref, g_ref, b_ref, wr_ref, br_ref,
                      x1_ref, rf_ref, ri_ref, wh_ref, wl_ref):
    @pl.when(pl.program_id(0) == 0)
    def _():
        hi, lo = _split_bf16(wr_ref[...])
        wh_ref[...] = hi
        wl_ref[...] = lo

    y = DEEPNORM_ALPHA * x_ref[...] + h_ref[...]
    mu = jnp.mean(y, axis=-1, keepdims=True)
    var = jnp.mean(jnp.square(y - mu), axis=-1, keepdims=True)
    x1 = (y - mu) * lax.rsqrt(var + LN_EPS) * g_ref[...] + b_ref[...]
    x1_ref[...] = x1

    xh, xl = _split_bf16(x1)
    wh = wh_ref[...]
    logits = (jnp.dot(xh, wh, preferred_element_type=f32)
              + jnp.dot(xl, wh, preferred_element_type=f32)
              + jnp.dot(xh, wl_ref[...], preferred_element_type=f32)) + br_ref[...]

    lane = lax.broadcasted_iota(i32, logits.shape, 1)
    lane_f = lane.astype(f32)
    neg_inf = jnp.float32(-jnp.inf)
    big = jnp.float32(1e9)

    def masked_argmax(mask):
        vals = jnp.where(mask, logits, neg_inf)
        best = jnp.max(vals, axis=-1, keepdims=True)
        idx = jnp.min(jnp.where(mask & (logits == best), lane_f, big), axis=-1, keepdims=True)
        return best, idx

    gmask = lane < N_GROUPS
    gmax, gidx = masked_argmax(gmask)
    denom = jnp.sum(jnp.where(gmask, jnp.exp(logits - gmax), 0.0), axis=-1, keepdims=True)
    p_group = 1.0 / denom

    first = N_GROUPS + EXPERTS_PER_GROUP * gidx
    emask = (lane_f >= first) & (lane_f < first + EXPERTS_PER_GROUP)
    v1, i1 = masked_argmax(emask)
    v2, i2 = masked_argmax(emask & (lane_f != i1))
    t = jnp.exp(v2 - v1)
    w0 = p_group * (1.0 / (1.0 + t))
    w1 = p_group * (t / (1.0 + t))

    rf_ref[...] = jnp.where(lane == 0, w0, jnp.where(lane == 1, w1, 0.0))
    e0 = (i1 - N_GROUPS).astype(i32)
    e1 = (i2 - N_GROUPS).astype(i32)
    ri_ref[...] = jnp.where(lane == 0, e0, jnp.where(lane == 1, e1, 0))


def ln_router(x, h, ln_g, ln_b, w_route, b_route, layer):
    row = pl.BlockSpec((LN_TM, D_MODEL), lambda i: (i, 0))
    vec = pl.BlockSpec((None, 1, D_MODEL), lambda i: (layer, 0, 0))
    small = pl.BlockSpec((LN_TM, LANES), lambda i: (i, 0))
    return pl.pallas_call(
        _ln_router_kernel,
        grid=(N_TOK // LN_TM,),
        in_specs=[row, row, vec, vec,
                  pl.BlockSpec((None, D_MODEL, LANES), lambda i: (layer, 0, 0)),
                  pl.BlockSpec((None, 1, LANES), lambda i: (layer, 0, 0))],
        out_specs=[row, small, small],
        out_shape=[jax.ShapeDtypeStruct((N_TOK, D_MODEL), f32),
                   jax.ShapeDtypeStruct((N_TOK, LANES), f32),
                   jax.ShapeDtypeStruct((N_TOK, LANES), i32)],
        scratch_shapes=[pltpu.VMEM((D_MODEL, LANES), bf16), pltpu.VMEM((D_MODEL, LANES), bf16)],
        compiler_params=_params("arbitrary"),
        name="ln_router",
    )(x, h, ln_g, ln_b, w_route, b_route)


def _plan_kernel(e0_ref, e1_ref, d0_ref, d1_ref, info_ref):
    e0 = e0_ref[...]
    e1 = e1_ref[...]
    r = lax.broadcasted_iota(i32, (LANES, LANES), 0)
    c = lax.broadcasted_iota(i32, (LANES, LANES), 1)
    incl = (r <= c).astype(bf16)
    ones_l = jnp.ones((LANES, LANES), bf16)
    rr = lax.broadcasted_iota(i32, (TOK_ROWS, TOK_ROWS), 0)
    cc = lax.broadcasted_iota(i32, (TOK_ROWS, TOK_ROWS), 1)
    before = (cc < rr).astype(bf16)
    ones_r = jnp.ones((TOK_ROWS, TOK_ROWS), bf16)
    tile_start = (lax.broadcasted_iota(i32, (8, LANES), 1) * EXP_TM).astype(f32)

    off = jnp.zeros((TOK_ROWS, LANES), f32)
    d0 = jnp.zeros((TOK_ROWS, LANES), f32)
    d1 = jnp.zeros((TOK_ROWS, LANES), f32)
    tile_expert = jnp.zeros((8, LANES), f32)
    seg_end = jnp.zeros((8, LANES), f32)
    lane8 = lax.broadcasted_iota(i32, (8, LANES), 1)
    for e in range(N_EXPERTS):
        m0 = e0 == e
        m1 = e1 == e
        m = (m0 | m1).astype(f32).astype(bf16)
        within = jnp.dot(m, incl, preferred_element_type=f32)
        row_sum = jnp.dot(m, ones_l, preferred_element_type=f32).astype(bf16)
        above = jnp.dot(before, row_sum, preferred_element_type=f32)
        total = jnp.dot(ones_r, row_sum, preferred_element_type=f32)
        pos = off + above + within - 1.0
        d0 = jnp.where(m0, pos, d0)
        d1 = jnp.where(m1, pos, d1)
        padded = jnp.floor((total + (EXP_TM - 1)) * (1.0 / EXP_TM)) * EXP_TM
        off = off + padded
        tile_expert = tile_expert + (tile_start >= off[0:8, :]).astype(f32)
        seg_end = jnp.where(lane8 == e, off[0:8, :] * (1.0 / EXP_TM), seg_end)
    d0_ref[...] = d0.astype(i32)
    d1_ref[...] = d1.astype(i32)
    n_tiles = off[0:8, :] * (1.0 / EXP_TM)
    sub = lax.broadcasted_iota(i32, (8, LANES), 0)
    info = jnp.where(sub == 0, jnp.minimum(tile_expert, N_EXPERTS - 1.0),
                     jnp.where(sub == 1, n_tiles, seg_end))
    info_ref[...] = info.astype(i32)


def routing_plan(e0, e1):
    full = pl.BlockSpec((TOK_ROWS, LANES), lambda: (0, 0))
    return pl.pallas_call(
        _plan_kernel,
        in_specs=[full, full],
        out_specs=[full, full, pl.BlockSpec((8, LANES), lambda: (0, 0))],
        out_shape=[jax.ShapeDtypeStruct((TOK_ROWS, LANES), i32),
                   jax.ShapeDtypeStruct((TOK_ROWS, LANES), i32),
                   jax.ShapeDtypeStruct((8, LANES), i32)],
        name="routing_plan",
    )(e0, e1)


def _dispatch_kernel(d0_ref, d1_ref, x_ref, prev_hbm, xs_hbm, sem):
    del prev_hbm
    base = pl.program_id(0) * LN_TM

    def row(r, c):
        t = base + r
        src = x_ref.at[pl.ds(r, 1)]
        pltpu.make_async_copy(src, xs_hbm.at[pl.ds(d0_ref[t], 1)], sem).start()
        pltpu.make_async_copy(src, xs_hbm.at[pl.ds(d1_ref[t], 1)], sem).start()
        return c

    lax.fori_loop(0, LN_TM, row, 0, unroll=8)
    for _ in range(TOP_K):
        pltpu.make_async_copy(x_ref, xs_hbm.at[pl.ds(0, LN_TM)], sem).wait()


def dispatch(d0, d1, x1, prev_sorted):
    smem = pl.BlockSpec(memory_space=pltpu.SMEM)
    hbm = pl.BlockSpec(memory_space=pl.ANY)
    return pl.pallas_call(
        _dispatch_kernel,
        grid=(N_TOK // LN_TM,),
        in_specs=[smem, smem, pl.BlockSpec((LN_TM, D_MODEL), lambda i: (i, 0)), hbm],
        out_specs=hbm,
        out_shape=jax.ShapeDtypeStruct((SORTED_ROWS, D_MODEL), f32),
        scratch_shapes=[pltpu.SemaphoreType.DMA(())],
        input_output_aliases={3: 0},
        compiler_params=_params("arbitrary"),
        name="moe_dispatch",
    )(d0, d1, x1, prev_sorted)


def _expert_kernel(te_ref, nv_ref, seg_end_ref, xs_ref, wg_hbm, wu_hbm, wd_hbm, o_ref,
                   wg_f, wu_f, wd_f, wg_bf, wu_bf, wd_bf, seg_ref, sem, *, base):
    i = pl.program_id(0)
    n_valid = nv_ref[0]

    def weight_copies(e, slot):
        return [pltpu.make_async_copy(src.at[base + e], dst.at[slot], sem.at[slot, k])
                for k, (src, dst) in enumerate(((wg_hbm, wg_f), (wu_hbm, wu_f), (wd_hbm, wd_f)))]

    @pl.when(i < n_valid)
    def _():
        e = te_ref[i]
        prev = te_ref[jnp.maximum(i - 1, 0)]

        @pl.when(i == 0)
        def _():
            seg_ref[0] = 0
            for cp in weight_copies(e, 0):
                cp.start()

        @pl.when((i == 0) | (e != prev))
        def _():
            @pl.when(i > 0)
            def _():
                seg_ref[0] = seg_ref[0] + 1

            slot = seg_ref[0] % 2
            for cp in weight_copies(e, slot):
                cp.wait()
            nxt = seg_end_ref[e]

            @pl.when(nxt < n_valid)
            def _():
                for cp in weight_copies(te_ref[nxt], 1 - slot):
                    cp.start()

            wg_bf[...] = wg_f[slot].astype(bf16)
            wu_bf[...] = wu_f[slot].astype(bf16)
            wd_bf[...] = wd_f[slot].astype(bf16)

        x = xs_ref[...].astype(bf16)
        gate = jnp.dot(x, wg_bf[...], preferred_element_type=f32)
        up = jnp.dot(x, wu_bf[...], preferred_element_type=f32)
        h = jax.nn.silu(gate) * up
        o_ref[...] = jnp.dot(h.astype(bf16), wd_bf[...], preferred_element_type=f32)

    @pl.when(i >= nv_ref[0])
    def _():
        o_ref[...] = jnp.zeros_like(o_ref)


def expert_mlp(tile_expert, n_valid, seg_end, xs, w_gate, w_up, w_down, layer):
    def row_map(i, te, nv, se):
        return (jnp.minimum(i, nv[0] - 1), 0)

    hbm = pl.BlockSpec(memory_space=pl.ANY)
    grid_spec = pltpu.PrefetchScalarGridSpec(
        num_scalar_prefetch=3,
        grid=(MAX_TILES,),
        in_specs=[pl.BlockSpec((EXP_TM, D_MODEL), row_map), hbm, hbm, hbm],
        out_specs=pl.BlockSpec((EXP_TM, D_MODEL), lambda i, te, nv, se: (i, 0)),
        scratch_shapes=[pltpu.VMEM((2, D_MODEL, D_EXPERT), f32),
                        pltpu.VMEM((2, D_MODEL, D_EXPERT), f32),
                        pltpu.VMEM((2, D_EXPERT, D_MODEL), f32),
                        pltpu.VMEM((D_MODEL, D_EXPERT), bf16),
                        pltpu.VMEM((D_MODEL, D_EXPERT), bf16),
                        pltpu.VMEM((D_EXPERT, D_MODEL), bf16),
                        pltpu.SMEM((1,), i32),
                        pltpu.SemaphoreType.DMA((2, 3))],
    )
    return pl.pallas_call(
        functools.partial(_expert_kernel, base=layer * N_EXPERTS),
        grid_spec=grid_spec,
        out_shape=jax.ShapeDtypeStruct((SORTED_ROWS, D_MODEL), f32),
        compiler_params=_params("arbitrary"),
        name="moe_experts",
    )(tile_expert, n_valid, seg_end, xs, w_gate, w_up, w_down)


def _combine_kernel(d0_ref, d1_ref, ys_hbm, x1_ref, rf_ref, g_ref, b_ref, o_ref, obf_ref,
                    buf, sem):
    s = pl.program_id(0)
    n_steps = pl.num_programs(0)
    slot = s % 2

    def issue(step, slot_):
        def row(r, c):
            t = step * LN_TM + r
            pltpu.make_async_copy(ys_hbm.at[pl.ds(d0_ref[t], 1)],
                                  buf.at[slot_, 0, pl.ds(r, 1)], sem.at[slot_]).start()
            pltpu.make_async_copy(ys_hbm.at[pl.ds(d1_ref[t], 1)],
                                  buf.at[slot_, 1, pl.ds(r, 1)], sem.at[slot_]).start()
            return c

        lax.fori_loop(0, LN_TM, row, 0, unroll=8)

    @pl.when(s == 0)
    def _():
        issue(0, 0)

    @pl.when(s + 1 < n_steps)
    def _():
        issue(s + 1, 1 - slot)

    for k in range(TOP_K):
        pltpu.make_async_copy(ys_hbm.at[pl.ds(0, LN_TM)], buf.at[slot, k], sem.at[slot]).wait()

    rf = rf_ref[...]
    y = rf[:, 0:1] * buf[slot, 0] + rf[:, 1:2] * buf[slot, 1]
    y = DEEPNORM_ALPHA * x1_ref[...] + y
    mu = jnp.mean(y, axis=-1, keepdims=True)
    var = jnp.mean(jnp.square(y - mu), axis=-1, keepdims=True)
    out = (y - mu) * lax.rsqrt(var + LN_EPS) * g_ref[...] + b_ref[...]
    o_ref[...] = out
    obf_ref[...] = out.astype(bf16)


def combine_ln(d0, d1, ys, x1, route_f, ln_g, ln_b, layer):
    smem = pl.BlockSpec(memory_space=pltpu.SMEM)
    row = pl.BlockSpec((LN_TM, D_MODEL), lambda i: (i, 0))
    vec = pl.BlockSpec((None, 1, D_MODEL), lambda i: (layer, 0, 0))
    return pl.pallas_call(
        _combine_kernel,
        grid=(N_TOK // LN_TM,),
        in_specs=[smem, smem, pl.BlockSpec(memory_space=pl.ANY), row,
                  pl.BlockSpec((LN_TM, LANES), lambda i: (i, 0)), vec, vec],
        out_specs=[row, row],
        out_shape=[jax.ShapeDtypeStruct((N_TOK, D_MODEL), f32),
                   jax.ShapeDtypeStruct((N_TOK, D_MODEL), bf16)],
        scratch_shapes=[pltpu.VMEM((2, TOP_K, LN_TM, D_MODEL), f32),
                        pltpu.SemaphoreType.DMA((2,))],
        compiler_params=_params("arbitrary"),
        name="moe_combine_ln",
    )(d0, d1, ys, x1, route_f, ln_g, ln_b)


def kernel(x, sb_w_qkv, sb_w_o, gm_w_in, gm_v_ln_g, gm_v_ln_b, gm_w_spatial, gm_b_spatial, gm_w_out, mix_ln_g, mix_ln_b, moe_w_group_router, moe_b_group_router, moe_w_expert_router, moe_b_expert_router, moe_w_gate, moe_w_up, moe_w_down, ffn_ln_g, ffn_ln_b):
    x = x.reshape(N_TOK, D_MODEL)

    vec3 = lambda a: a.reshape(a.shape[0], 1, a.shape[1])
    gm_g3, gm_b3 = vec3(gm_v_ln_g), vec3(gm_v_ln_b)
    gm_bs_t = jnp.swapaxes(gm_b_spatial, 1, 2)
    mix_g3, mix_b3 = vec3(mix_ln_g), vec3(mix_ln_b)
    ffn_g3, ffn_b3 = vec3(ffn_ln_g), vec3(ffn_ln_b)
    w_er = jnp.transpose(moe_w_expert_router, (0, 2, 1, 3)).reshape(DEPTH, D_MODEL, N_EXPERTS)
    w_route = jnp.concatenate([moe_w_group_router, w_er], axis=-1)
    w_route = jnp.pad(w_route, ((0, 0), (0, 0), (0, LANES - N_ROUTE_COLS)))
    b_route = jnp.concatenate([moe_b_group_router, moe_b_expert_router.reshape(DEPTH, N_EXPERTS)], axis=-1)
    b_route = jnp.pad(b_route, ((0, 0), (0, LANES - N_ROUTE_COLS))).reshape(DEPTH, 1, LANES)
    w_gate = moe_w_gate.reshape(DEPTH * N_EXPERTS, D_MODEL, D_EXPERT)
    w_up = moe_w_up.reshape(DEPTH * N_EXPERTS, D_MODEL, D_EXPERT)
    w_down = moe_w_down.reshape(DEPTH * N_EXPERTS, D_EXPERT, D_MODEL)

    r = lax.broadcasted_iota(i32, (ATT_TK, ATT_TK), 0)
    c = lax.broadcasted_iota(i32, (ATT_TK, ATT_TK), 1)
    tri = (r > c).astype(bf16)
    xs = jnp.zeros((SORTED_ROWS, D_MODEL), f32)

    x_mxu = x
    for i in range(DEPTH):
        j = i // N_MIXERS
        if i % N_MIXERS == 0:
            qkv = matmul_wcast(x_mxu, sb_w_qkv, j, tm=512, tn=1024, out_dtype=bf16, name="sb_qkv")
            o = stick_breaking_attention(qkv, tri)
            h = matmul_wcast(o, sb_w_o, j, tm=512, tn=1024, out_dtype=f32, name="sb_out")
        else:
            z = matmul_wcast(x_mxu, gm_w_in, j, tm=512, tn=1024, out_dtype=bf16, act="gelu", name="gm_in")
            gated = spatial_gate(z, gm_g3, gm_b3, gm_w_spatial, gm_bs_t, j)
            h = matmul_wcast(gated, gm_w_out, j, tm=256, tn=512, out_dtype=f32, name="gm_out")
        x1, route_f, route_i = ln_router(x, h, mix_g3, mix_b3, w_route, b_route, i)
        e0 = route_i[:, 0].reshape(TOK_ROWS, LANES)
        e1 = route_i[:, 1].reshape(TOK_ROWS, LANES)
        d0, d1, info = routing_plan(e0, e1)
        d0 = d0.reshape(N_TOK)
        d1 = d1.reshape(N_TOK)
        xs = dispatch(d0, d1, x1, xs)
        ys = expert_mlp(info[0], info[1, :1], info[2], xs, w_gate, w_up, w_down, i)
        x, x_mxu = combine_ln(d0, d1, ys, x1, route_f, ffn_g3, ffn_b3, i)
    return x.reshape(BATCH, SEQ, D_MODEL)
```
